```python
import math
import jax, jax.numpy as jnp
from jax import lax
import numpy as np

D_MODEL = 1024
BATCH = 16
SEQ = 4096
DEPTH = 4

N_MIXERS = 2
N_ATTN_LAYERS = (DEPTH + 1) // 2
N_POOL_LAYERS = DEPTH // 2
N_HEADS = 8
HEAD_DIM = D_MODEL // N_HEADS // 2
V_DIM = 2 * HEAD_DIM
ATTN_SCALE = HEAD_DIM ** -0.5
Q_BLOCK = 128
POOL_WINDOWS = (2, 4, 8, 16)
N_POOL_GROUPS = len(POOL_WINDOWS)
POOL_GROUP_DIM = D_MODEL // N_POOL_GROUPS
D_FF = 2816
CONV_WIDTH = 3
PLE_DIM = 256
RMS_EPS = 1e-6

kernel_name = "hybrid_diffattn_multipool_convglu_ple"


def rmsnorm(x, g):
    xf = x.astype(jnp.float32)
    inv = lax.rsqrt(jnp.mean(xf * xf, axis=-1, keepdims=True) + RMS_EPS)
    return (xf * inv).astype(x.dtype) * g


def lambda_init_fn(layer_idx):
    return 0.8 - 0.6 * math.exp(-0.3 * layer_idx)


def diff_attention(xn, w_qkv, lam_q1, lam_k1, lam_q2, lam_k2, sub_g, w_o, lambda_init):
    B, S, _ = xn.shape
    qkv = xn @ w_qkv
    q, k, v = jnp.split(qkv, 3, axis=-1)
    q = q.reshape(B, S, N_HEADS, 2, HEAD_DIM).transpose(0, 2, 3, 1, 4)
    k = k.reshape(B, S, N_HEADS, 2, HEAD_DIM).transpose(0, 2, 3, 1, 4)
    v = v.reshape(B, S, N_HEADS, V_DIM).transpose(0, 2, 1, 3)
    lam = (jnp.exp(jnp.sum(lam_q1.astype(jnp.float32) * lam_k1.astype(jnp.float32)))
           - jnp.exp(jnp.sum(lam_q2.astype(jnp.float32) * lam_k2.astype(jnp.float32)))
           + lambda_init)
    n_blocks = S // Q_BLOCK
    q_blocks = q.reshape(B, N_HEADS, 2, n_blocks, Q_BLOCK, HEAD_DIM).transpose(3, 0, 1, 2, 4, 5)
    k_pos = jnp.arange(S)

    def one_block(args):
        q_blk, blk_idx = args
        s = jnp.einsum('bhcqd,bhckd->bhcqk', q_blk, k).astype(jnp.float32) * ATTN_SCALE
        q_pos = blk_idx * Q_BLOCK + jnp.arange(Q_BLOCK)
        causal = k_pos[None, :] <= q_pos[:, None]
        s = jnp.where(causal, s, -jnp.inf)
        a = jax.nn.softmax(s, axis=-1)
        w = (a[:, :, 0] - lam * a[:, :, 1]).astype(v.dtype)
        return jnp.einsum('bhqk,bhkv->bhqv', w, v)

    o = lax.map(one_block, (q_blocks, jnp.arange(n_blocks)))
    o = o.transpose(1, 0, 3, 2, 4).reshape(B, S, N_HEADS, V_DIM)
    o = rmsnorm(o, sub_g) * (1.0 - lambda_init)
    return o.reshape(B, S, N_HEADS * V_DIM) @ w_o


def multiscale_pool(xn, w_pool, b_pool, scale):
    B, S, D = xn.shape
    xf = xn.astype(jnp.float32)
    cs = jnp.cumsum(xf, axis=1)
    count_base = jnp.arange(1, S + 1, dtype=jnp.float32)[None, :, None]
    outs = []
    for g, win in enumerate(POOL_WINDOWS):
        sl = slice(g * POOL_GROUP_DIM, (g + 1) * POOL_GROUP_DIM)
        cs_g = cs[..., sl]
        lag = jnp.pad(cs_g, ((0, 0), (win, 0), (0, 0)))[:, :S]
        mean = (cs_g - lag) / jnp.minimum(count_base, float(win))
        outs.append(mean - xf[..., sl])
    y = jnp.stack(outs, axis=2).astype(xn.dtype)
    y = jnp.einsum('bsgc,gce->bsge', y, w_pool) + b_pool
    return y.reshape(B, S, D) * scale


def conv_glu_ffn(xn, w_up, conv_w, conv_b, w_down):
    u = xn @ w_up
    u = lax.conv_general_dilated(
        u, conv_w[:, None, :].astype(u.dtype), window_strides=(1,),
        padding=((CONV_WIDTH - 1, 0),),
        dimension_numbers=('NWC', 'WIO', 'NWC'),
        feature_group_count=2 * D_FF) + conv_b
    gate, up = jnp.split(u, 2, axis=-1)
    return (jax.nn.silu(gate) * up) @ w_down


def setup_inputs(seed: int = 0) -> dict:
    key = jax.random.key(seed)
    ks = jax.random.split(key, 24)
    f32 = jnp.float32

    def nrm(k, shape, scale):
        return jax.random.normal(k, shape, f32) * scale

    def gain(k, shape):
        return 1.0 + 0.05 * jax.random.normal(k, shape, f32)

    return {
        "x": nrm(ks[0], (BATCH, SEQ, D_MODEL), 1.0),
        "p": nrm(ks[1], (DEPTH, BATCH, SEQ, PLE_DIM), 1.0),
        "mix_norm_g": gain(ks[2], (DEPTH, D_MODEL)),
        "w_qkv": nrm(ks[3], (N_ATTN_LAYERS, D_MODEL, 3 * D_MODEL), D_MODEL ** -0.5),
        "lam_q1": nrm(ks[4], (N_ATTN_LAYERS, HEAD_DIM), 0.1),
        "lam_k1": nrm(ks[5], (N_ATTN_LAYERS, HEAD_DIM), 0.1),
        "lam_q2": nrm(ks[6], (N_ATTN_LAYERS, HEAD_DIM), 0.1),
        "lam_k2": nrm(ks[7], (N_ATTN_LAYERS, HEAD_DIM), 0.1),
        "attn_sub_g": gain(ks[8], (N_ATTN_LAYERS, V_DIM)),
        "w_attn_out": nrm(ks[9], (N_ATTN_LAYERS, N_HEADS * V_DIM, D_MODEL), (N_HEADS * V_DIM) ** -0.5),
        "w_pool": nrm(ks[10], (N_POOL_LAYERS, N_POOL_GROUPS, POOL_GROUP_DIM, POOL_GROUP_DIM), POOL_GROUP_DIM ** -0.5),
        "b_pool": nrm(ks[11], (N_POOL_LAYERS, N_POOL_GROUPS, POOL_GROUP_DIM), 0.02),
        "pool_scale": 0.5 + 0.05 * jax.random.normal(ks[12], (N_POOL_LAYERS, D_MODEL), f32),
        "ffn_norm_g": gain(ks[13], (DEPTH, D_MODEL)),
        "w_up": nrm(ks[14], (DEPTH, D_MODEL, 2 * D_FF), D_MODEL ** -0.5),
        "conv_w": nrm(ks[15], (DEPTH, CONV_WIDTH, 2 * D_FF), CONV_WIDTH ** -0.5),
        "conv_b": nrm(ks[16], (DEPTH, 2 * D_FF), 0.02),
        "w_down": nrm(ks[17], (DEPTH, D_FF, D_MODEL), D_FF ** -0.5),
        "ple_norm_g": gain(ks[18], (DEPTH, D_MODEL)),
        "w_ple_gate": nrm(ks[19], (DEPTH, D_MODEL, D_MODEL), D_MODEL ** -0.5),
        "w_ple_proj": nrm(ks[20], (DEPTH, PLE_DIM, D_MODEL), PLE_DIM ** -0.5),
        "final_norm_g": gain(ks[21], (D_MODEL,)),
    }


def reference(x, p, mix_norm_g, w_qkv, lam_q1, lam_k1, lam_q2, lam_k2, attn_sub_g, w_attn_out,
              w_pool, b_pool, pool_scale, ffn_norm_g, w_up, conv_w, conv_b, w_down,
              ple_norm_g, w_ple_gate, w_ple_proj, final_norm_g):
    h = x
    attn_idx = 0
    pool_idx = 0
    for i in range(DEPTH):
        hn = rmsnorm(h, mix_norm_g[i])
        if i % N_MIXERS == 0:
            mix = diff_attention(hn, w_qkv[attn_idx], lam_q1[attn_idx], lam_k1[attn_idx],
                                 lam_q2[attn_idx], lam_k2[attn_idx], attn_sub_g[attn_idx],
                                 w_attn_out[attn_idx], lambda_init_fn(i))
            attn_idx += 1
        else:
            mix = multiscale_pool(hn, w_pool[pool_idx], b_pool[pool_idx], pool_scale[pool_idx])
            pool_idx += 1
        h = h + mix
        h = h + conv_glu_ffn(rmsnorm(h, ffn_norm_g[i]), w_up[i], conv_w[i], conv_b[i], w_down[i])
        gate = jax.nn.sigmoid(rmsnorm(h, ple_norm_g[i]) @ w_ple_gate[i])
        h = h + gate * (p[i] @ w_ple_proj[i])
    return rmsnorm(h, final_norm_g)
```

```python
import functools
import math

import jax
import jax.numpy as jnp
from jax import lax
from jax.experimental import pallas as pl
from jax.experimental.pallas import tpu as pltpu

N_HEADS = 8
HEAD_DIM = 64
V_DIM = 2 * HEAD_DIM
ATTN_SCALE = HEAD_DIM ** -0.5
POOL_WINDOWS = (2, 4, 8, 16)
CONV_WIDTH = 3
RMS_EPS = 1e-6

SEQ_TILE = 512
FF_CHUNK = 256
SUBLANES = 8
CONV_HALO = SUBLANES
POOL_HALO = 2 * SUBLANES
VMEM_LIMIT_BYTES = 56 * 1024 * 1024

F32 = jnp.float32
BF16 = jnp.bfloat16
_NT = (((1,), (1,)), ((), ()))


def _lambda_init(layer_idx):
    return 0.8 - 0.6 * math.exp(-0.3 * layer_idx)


def _rmsnorm(x, g):
    inv = lax.rsqrt(jnp.mean(x * x, axis=-1, keepdims=True) + RMS_EPS)
    return x * inv * g


def _shift_rows(x, k):
    return pltpu.roll(x, k, 0)


def _resident(shape):
    zeros = (0,) * len(shape)
    return pl.BlockSpec(shape, lambda *_: zeros, pipeline_mode=pl.Buffered(1))


def _params(n_grid_dims):
    return pltpu.CompilerParams(
        dimension_semantics=("arbitrary",) * n_grid_dims,
        vmem_limit_bytes=VMEM_LIMIT_BYTES)


def _qkv_kernel(h_ref, g_ref, wqT_ref, wk_ref, wvT_ref, qT_ref, k_ref, vT_ref):
    hn = _rmsnorm(h_ref[...], g_ref[...]).astype(BF16)
    k_ref[...] = jnp.dot(hn, wk_ref[...], preferred_element_type=F32).astype(BF16)
    qT_ref[...] = lax.dot_general(wqT_ref[...], hn, _NT, preferred_element_type=F32).astype(BF16)
    vT_ref[...] = lax.dot_general(wvT_ref[...], hn, _NT, preferred_element_type=F32).astype(BF16)


def _qkv_proj(h, g, wqT, wk, wvT):
    n, d = h.shape
    t = SEQ_TILE
    nb = n // t
    return pl.pallas_call(
        _qkv_kernel,
        grid=(nb,),
        in_specs=[
            pl.BlockSpec((t, d), lambda i: (i, 0)),
            _resident((1, d)),
            _resident((d, d)), _resident((d, d)), _resident((d, d)),
        ],
        out_specs=[
            pl.BlockSpec((None, d, t), lambda i: (i, 0, 0)),
            pl.BlockSpec((t, d), lambda i: (i, 0)),
            pl.BlockSpec((None, d, t), lambda i: (i, 0, 0)),
        ],
        out_shape=[
            jax.ShapeDtypeStruct((nb, d, t), BF16),
            jax.ShapeDtypeStruct((n, d), BF16),
            jax.ShapeDtypeStruct((nb, d, t), BF16),
        ],
        compiler_params=_params(1),
        name="qkv_proj",
    )(h, g, wqT, wk, wvT)


def _attn_kernel(lq1_ref, lk1_ref, lq2_ref, lk2_ref, subg_ref, qT_ref, k_ref, vT_ref, o_ref,
                 m_ref, l_ref, acc_ref, *, lambda_init):
    t = SEQ_TILE
    qi = pl.program_id(2)
    qT = qT_ref[...]
    half = lax.broadcasted_iota(jnp.int32, qT.shape, 0) < HEAD_DIM
    zero = jnp.zeros_like(qT)
    qTs = (jnp.where(half, qT, zero), jnp.where(half, zero, qT))

    kb = k_ref[pl.ds(pl.multiple_of(qi * t, t), t), :]
    vb = vT_ref[qi]
    key_pos = lax.broadcasted_iota(jnp.int32, (t, t), 0)
    qry_pos = lax.broadcasted_iota(jnp.int32, (t, t), 1)
    causal = key_pos <= qry_pos
    for c in range(2):
        s = jnp.dot(kb, qTs[c], preferred_element_type=F32)
        s = jnp.where(causal, s, -jnp.inf)
        m = jnp.max(s, axis=0, keepdims=True)
        p = jnp.exp(s - m)
        m_ref[c] = m
        l_ref[c] = jnp.sum(p, axis=0, keepdims=True)
        acc_ref[c] = jnp.dot(vb, p.astype(BF16), preferred_element_type=F32)

    def kv_step(ki, carry):
        kb = k_ref[pl.ds(pl.multiple_of(ki * t, t), t), :]
        vb = vT_ref[ki]
        for c in range(2):
            s = jnp.dot(kb, qTs[c], preferred_element_type=F32)
            m_old = m_ref[c]
            m_new = jnp.maximum(m_old, jnp.max(s, axis=0, keepdims=True))
            alpha = jnp.exp(m_old - m_new)
            p = jnp.exp(s - m_new)
            m_ref[c] = m_new
            l_ref[c] = alpha * l_ref[c] + jnp.sum(p, axis=0, keepdims=True)
            acc_ref[c] = alpha * acc_ref[c] + jnp.dot(vb, p.astype(BF16), preferred_element_type=F32)
        return carry

    lax.fori_loop(0, qi, kv_step, 0)

    lam = (jnp.exp(jnp.sum(lq1_ref[...] * lk1_ref[...], keepdims=True))
           - jnp.exp(jnp.sum(lq2_ref[...] * lk2_ref[...], keepdims=True)) + lambda_init)
    oT = acc_ref[0] / l_ref[0] - lam * (acc_ref[1] / l_ref[1])
    inv = lax.rsqrt(jnp.mean(oT * oT, axis=0, keepdims=True) + RMS_EPS)
    oT = oT * inv * subg_ref[...] * (1.0 - lambda_init)
    o_ref[...] = oT.T.astype(BF16)


def _diff_attention(qT, k, vT, lq1, lk1, lq2, lk2, sub_g, *, batch, lambda_init):
    nb, d, t = qT.shape
    nq = nb // batch
    seq = nq * t
    k3 = k.reshape(batch, seq, d)
    vT4 = vT.reshape(batch, nq, d, t)
    lam_spec = pl.BlockSpec((1, HEAD_DIM), lambda b, h, q: (0, 0))
    return pl.pallas_call(
        functools.partial(_attn_kernel, lambda_init=lambda_init),
        grid=(batch, N_HEADS, nq),
        in_specs=[
            lam_spec, lam_spec, lam_spec, lam_spec,
            pl.BlockSpec((V_DIM, 1), lambda b, h, q: (0, 0)),
            pl.BlockSpec((None, V_DIM, t), lambda b, h, q: (b * nq + q, h, 0)),
            pl.BlockSpec((None, seq, V_DIM), lambda b, h, q: (b, 0, h)),
            pl.BlockSpec((None, nq, V_DIM, t), lambda b, h, q: (b, 0, h, 0)),
        ],
        out_specs=pl.BlockSpec((t, V_DIM), lambda b, h, q: (b * nq + q, h)),
        out_shape=jax.ShapeDtypeStruct((batch * seq, d), BF16),
        scratch_shapes=[
            pltpu.VMEM((2, 1, t), F32),
            pltpu.VMEM((2, 1, t), F32),
            pltpu.VMEM((2, V_DIM, t), F32),
        ],
        compiler_params=_params(3),
        name="diff_attention",
    )(lq1, lk1, lq2, lk2, sub_g, qT, k3, vT4)


def _attn_out_kernel(h_ref, o_ref, wo_ref, out_ref):
    out_ref[...] = h_ref[...] + jnp.dot(o_ref[...], wo_ref[...], preferred_element_type=F32)


def _attn_out(h, o, wo):
    n, d = h.shape
    t = SEQ_TILE
    row = pl.BlockSpec((t, d), lambda i: (i, 0))
    return pl.pallas_call(
        _attn_out_kernel,
        grid=(n // t,),
        in_specs=[row, row, _resident((d, d))],
        out_specs=row,
        out_shape=jax.ShapeDtypeStruct((n, d), F32),
        compiler_params=_params(1),
        name="attn_out",
    )(h, o, wo)


def _pool_kernel(hprev_ref, h_ref, g_ref, w_ref, b_ref, sc_ref, out_ref, *, blocks_per_seq):
    t = SEQ_TILE
    blk = pl.program_id(0) % blocks_per_seq
    h = h_ref[...]
    halo = jnp.where(blk == 0, 0.0, hprev_ref[...])
    xn = _rmsnorm(jnp.concatenate([halo, h], axis=0), g_ref[...])
    pos = blk * t + lax.broadcasted_iota(jnp.int32, (t, 1), 0)
    gc = xn.shape[1] // len(POOL_WINDOWS)
    for g, win in enumerate(POOL_WINDOWS):
        sl = slice(g * gc, (g + 1) * gc)
        xg = xn[:, sl]
        acc = xg
        span = 1
        while span < win:
            acc = acc + _shift_rows(acc, span)
            span *= 2
        count = jnp.minimum(pos + 1, win).astype(F32)
        y = acc[POOL_HALO:] / count - xg[POOL_HALO:]
        z = jnp.dot(y.astype(BF16), w_ref[g], preferred_element_type=F32) + b_ref[g]
        out_ref[:, sl] = h[:, sl] + z * sc_ref[:, sl]


def _pool_mix(h, g, w, b, scale, *, seq):
    n, d = h.shape
    t = SEQ_TILE
    ng = len(POOL_WINDOWS)
    gc = d // ng
    halo_blocks = t // POOL_HALO
    return pl.pallas_call(
        functools.partial(_pool_kernel, blocks_per_seq=seq // t),
        grid=(n // t,),
        in_specs=[
            pl.BlockSpec((POOL_HALO, d), lambda i: (jnp.maximum(i * halo_blocks - 1, 0), 0)),
            pl.BlockSpec((t, d), lambda i: (i, 0)),
            _resident((1, d)),
            _resident((ng, gc, gc)),
            _resident((ng, 1, gc)),
            _resident((1, d)),
        ],
        out_specs=pl.BlockSpec((t, d), lambda i: (i, 0)),
        out_shape=jax.ShapeDtypeStruct((n, d), F32),
        compiler_params=_params(1),
        name="pool_mix",
    )(h, h, g, w, b, scale)


def _ffn_ple_kernel(hprev_ref, h_ref, p_ref, gf_ref, wup_ref, cw_ref, cb_ref, wdn_ref,
                    gp_ref, wg_ref, wp_ref, gfin_ref, out_ref, acc_ref,
                    *, blocks_per_seq, final_norm):
    blk = pl.program_id(0) % blocks_per_seq
    h = h_ref[...]
    halo = jnp.where(blk == 0, 0.0, hprev_ref[...])
    xn = _rmsnorm(jnp.concatenate([halo, h], axis=0), gf_ref[...]).astype(BF16)
    n_chunks, _, width = wup_ref.shape
    fc = width // 2
    acc_ref[...] = jnp.zeros_like(acc_ref)

    def chunk(j, carry):
        u = jnp.dot(xn, wup_ref[j], preferred_element_type=F32)
        cw = cw_ref[j]
        y = cb_ref[j] + cw[CONV_WIDTH - 1:] * u[CONV_HALO:]
        for back in range(1, CONV_WIDTH):
            tap = CONV_WIDTH - 1 - back
            y = y + cw[tap:tap + 1] * _shift_rows(u, back)[CONV_HALO:]
        gate, up = y[:, :fc], y[:, fc:]
        act = (gate * jax.nn.sigmoid(gate) * up).astype(BF16)
        acc_ref[...] += jnp.dot(act, wdn_ref[j], preferred_element_type=F32)
        return carry

    lax.fori_loop(0, n_chunks, chunk, 0)

    h = h + acc_ref[...]
    hn = _rmsnorm(h, gp_ref[...]).astype(BF16)
    gate = jax.nn.sigmoid(jnp.dot(hn, wg_ref[...], preferred_element_type=F32))
    h = h + gate * jnp.dot(p_ref[...].astype(BF16), wp_ref[...], preferred_element_type=F32)
    if final_norm:
        h = _rmsnorm(h, gfin_ref[...])
    out_ref[...] = h


def _ffn_ple(h, p, gf, wup, cw, cb, wdn, gp, wg, wp, gfin, *, seq, final_norm):
    n, d = h.shape
    t = SEQ_TILE
    pd = p.shape[1]
    halo_blocks = t // CONV_HALO
    return pl.pallas_call(
        functools.partial(_ffn_ple_kernel, blocks_per_seq=seq // t, final_norm=final_norm),
        grid=(n // t,),
        in_specs=[
            pl.BlockSpec((CONV_HALO, d), lambda i: (jnp.maximum(i * halo_blocks - 1, 0), 0)),
            pl.BlockSpec((t, d), lambda i: (i, 0)),
            pl.BlockSpec((t, pd), lambda i: (i, 0)),
            _resident((1, d)),
            _resident(wup.shape), _resident(cw.shape), _resident(cb.shape), _resident(wdn.shape),
            _resident((1, d)),
            _resident((d, d)), _resident((pd, d)),
            _resident((1, d)),
        ],
        out_specs=pl.BlockSpec((t, d), lambda i: (i, 0)),
        out_shape=jax.ShapeDtypeStruct((n, d), F32),
        scratch_shapes=[pltpu.VMEM((t, d), F32)],
        compiler_params=_params(1),
        name="ffn_ple",
    )(h, h, p, gf, wup, cw, cb, wdn, gp, wg, wp, gfin)


def _chunk_ffn_weights(w_up, conv_w, conv_b, w_down):
    d, two_f = w_up.shape
    f = two_f // 2
    nc = f // FF_CHUNK
    wup = w_up.reshape(d, 2, nc, FF_CHUNK).transpose(2, 0, 1, 3).reshape(nc, d, 2 * FF_CHUNK)
    cw = conv_w.reshape(CONV_WIDTH, 2, nc, FF_CHUNK).transpose(2, 0, 1, 3).reshape(nc, CONV_WIDTH, 2 * FF_CHUNK)
    cb = conv_b.reshape(2, nc, FF_CHUNK).transpose(1, 0, 2).reshape(nc, 1, 2 * FF_CHUNK)
    wdn = w_down.reshape(nc, FF_CHUNK, w_down.shape[1])
    return wup.astype(BF16), cw, cb, wdn.astype(BF16)


def kernel(x, p, mix_norm_g, w_qkv, lam_q1, lam_k1, lam_q2, lam_k2, attn_sub_g, w_attn_out, w_pool, b_pool, pool_scale, ffn_norm_g, w_up, conv_w, conv_b, w_down, ple_norm_g, w_ple_gate, w_ple_proj, final_norm_g):
    batch, seq, d = x.shape
    depth = p.shape[0]
    n = batch * seq
    assert seq % SEQ_TILE == 0 and d == N_HEADS * V_DIM and w_up.shape[2] % (2 * FF_CHUNK) == 0
    h = x.reshape(n, d)
    attn_idx = pool_idx = 0
    for i in range(depth):
        g_mix = mix_norm_g[i].reshape(1, d)
        if i % 2 == 0:
            a = attn_idx
            wq, wk, wv = jnp.split(w_qkv[a], 3, axis=1)
            qT, k, vT = _qkv_proj(h, g_mix, (wq * ATTN_SCALE).T.astype(BF16), wk.astype(BF16), wv.T.astype(BF16))
            o = _diff_attention(
                qT, k, vT,
                lam_q1[a].reshape(1, HEAD_DIM), lam_k1[a].reshape(1, HEAD_DIM),
                lam_q2[a].reshape(1, HEAD_DIM), lam_k2[a].reshape(1, HEAD_DIM),
                attn_sub_g[a].reshape(V_DIM, 1), batch=batch, lambda_init=_lambda_init(i))
            h = _attn_out(h, o, w_attn_out[a].astype(BF16))
            attn_idx += 1
        else:
            b = pool_idx
            ng, gc = b_pool.shape[1], b_pool.shape[2]
            h = _pool_mix(h, g_mix, w_pool[b].astype(BF16), b_pool[b].reshape(ng, 1, gc),
                          pool_scale[b].reshape(1, d), seq=seq)
            pool_idx += 1
        wup, cw, cb, wdn = _chunk_ffn_weights(w_up[i], conv_w[i], conv_b[i], w_down[i])
        h = _ffn_ple(h, p[i].reshape(n, p.shape[3]), ffn_norm_g[i].reshape(1, d), wup, cw, cb, wdn,
                     ple_norm_g[i].reshape(1, d), w_ple_gate[i].astype(BF16), w_ple_proj[i].astype(BF16),
                     final_norm_g.reshape(1, d), seq=seq, final_norm=(i == depth - 1))
    return h.reshape(batch, seq, d)
```
